```python
import math
import jax, jax.numpy as jnp
from jax import lax
import numpy as np

D_MODEL = 1024
BATCH = 4
SEQ = 8192
DEPTH = 2

HEAD_DIM = 64
N_SLOTS = 8
DILATED_GROUPS = ((128, 1), (512, 4), (2048, 16))
N_GROUPS = len(DILATED_GROUPS)
HALF_SPAN = 64
BLK = 64
ATT_WIDTH = N_SLOTS * HEAD_DIM
QKV_WIDTH = N_GROUPS * 3 * ATT_WIDTH
ALIBI_MAX = 8.0
POOL_WINDOWS = (2, 4, 8, 16)
POOL_GROUP = 128
POOL_WIDTH = len(POOL_WINDOWS) * POOL_GROUP
IN_WIDTH = QKV_WIDTH + POOL_WIDTH + 2 * D_MODEL
D_FF = -(-8 * D_MODEL // (3 * 256)) * 256
ALPHA = (2 * DEPTH) ** 0.25
BETA = (8 * DEPTH) ** -0.25
LN_EPS = 1e-5
N_MOD = 6

kernel_name = "hybrid_dilated_attn_pool_deepnorm_adaln"


def layer_norm(x, g=None, b=None):
    xf = x.astype(jnp.float32)
    mu = jnp.mean(xf, axis=-1, keepdims=True)
    var = jnp.mean(jnp.square(xf - mu), axis=-1, keepdims=True)
    y = (xf - mu) * lax.rsqrt(var + LN_EPS)
    if g is not None:
        y = y * g.astype(jnp.float32) + b.astype(jnp.float32)
    return y.astype(x.dtype)


def alibi_slopes():
    n = N_GROUPS * N_SLOTS
    i = jnp.arange(1, n + 1, dtype=jnp.float32)
    return jnp.exp2(-ALIBI_MAX * i / n).reshape(N_GROUPS, N_SLOTS)


def dilated_band_attention(q, k, v, rate, slopes):
    B, S, H, Dh = q.shape
    L = S // rate
    nb = -(-L // BLK)
    Lp = nb * BLK

    def phases(t):
        t = t.reshape(B, L, rate, H, Dh).transpose(0, 2, 3, 1, 4).reshape(B * rate, H, L, Dh)
        return jnp.pad(t, ((0, 0), (0, 0), (0, Lp - L), (0, 0)))

    def windows(t):
        t = jnp.pad(t, ((0, 0), (0, 0), (BLK, BLK), (0, 0))).reshape(B * rate, H, nb + 2, BLK, Dh)
        return jnp.concatenate([t[:, :, :-2], t[:, :, 1:-1], t[:, :, 2:]], axis=3)

    qb = phases(q).reshape(B * rate, H, nb, BLK, Dh)
    kw = windows(phases(k))
    vw = windows(phases(v))

    scores = jnp.einsum('bhnqd,bhnkd->bhnqk', qb, kw).astype(jnp.float32) * (Dh ** -0.5)
    a = jnp.arange(BLK)[:, None]
    cidx = jnp.arange(3 * BLK)[None, :]
    rel = cidx - BLK - a
    kpos = (jnp.arange(nb)[:, None] - 1) * BLK + jnp.arange(3 * BLK)[None, :]
    valid = (jnp.abs(rel) <= HALF_SPAN)[None] & ((kpos >= 0) & (kpos < L))[:, None, :]
    dist = (rate * jnp.abs(rel)).astype(jnp.float32)
    scores = scores - slopes.astype(jnp.float32)[:, None, None, None] * dist[None, None]
    scores = jnp.where(valid, scores, -1e30)
    lse = jax.nn.logsumexp(scores, axis=-1)
    p = jnp.exp(scores - lse[..., None])
    out = jnp.einsum('bhnqk,bhnkd->bhnqd', p.astype(vw.dtype), vw)

    out = out.reshape(B * rate, H, Lp, Dh)[:, :, :L]
    out = out.reshape(B, rate, H, L, Dh).transpose(0, 3, 1, 2, 4).reshape(B, S, H, Dh)
    lse = lse.reshape(B * rate, H, Lp)[:, :, :L]
    lse = lse.reshape(B, rate, H, L).transpose(0, 3, 1, 2).reshape(B, S, H)
    return out, lse


def centred_pool_minus_identity(u, window):
    B, S, C = u.shape
    cs = jnp.concatenate([jnp.zeros((B, 1, C), jnp.float32),
                          jnp.cumsum(u.astype(jnp.float32), axis=1)], axis=1)
    t = jnp.arange(S)
    lo = jnp.clip(t - window // 2, 0, S)
    hi = jnp.clip(t + window // 2, 0, S)
    mean = (cs[:, hi] - cs[:, lo]) / (hi - lo).astype(jnp.float32)[None, :, None]
    return (mean - u.astype(jnp.float32)).astype(u.dtype)


def setup_inputs(seed: int = 0) -> dict:
    key = jax.random.key(seed)
    ks = jax.random.split(key, 20)
    f32 = jnp.float32
    nrm = lambda k, shape, s: jax.random.normal(k, shape, f32) * s
    L = DEPTH
    return {
        "x": nrm(ks[0], (BATCH, SEQ, D_MODEL), 1.0),
        "c": nrm(ks[1], (BATCH, D_MODEL), 1.0),
        "w_ada": nrm(ks[2], (L, D_MODEL, N_MOD * D_MODEL), 0.2 * D_MODEL ** -0.5),
        "b_ada": nrm(ks[3], (L, N_MOD * D_MODEL), 0.01),
        "w_in": nrm(ks[4], (L, D_MODEL, IN_WIDTH), D_MODEL ** -0.5),
        "w_pool_mix": nrm(ks[5], (L, len(POOL_WINDOWS), POOL_GROUP, POOL_GROUP), POOL_GROUP ** -0.5),
        "pool_scale": 1.0 + nrm(ks[6], (L, POOL_WIDTH), 0.1),
        "w_att_out": nrm(ks[7], (L, ATT_WIDTH, D_MODEL), ATT_WIDTH ** -0.5),
        "w_pool_out": nrm(ks[8], (L, POOL_WIDTH, D_MODEL), POOL_WIDTH ** -0.5),
        "w_o": nrm(ks[9], (L, D_MODEL, D_MODEL), BETA * D_MODEL ** -0.5),
        "ln1_g": 1.0 + nrm(ks[10], (L, D_MODEL), 0.05),
        "ln1_b": nrm(ks[11], (L, D_MODEL), 0.02),
        "w_ffn_in": nrm(ks[12], (L, D_MODEL, 2 * D_FF), D_MODEL ** -0.5),
        "w_ffn_out": nrm(ks[13], (L, D_FF, D_MODEL), BETA * D_FF ** -0.5),
        "ln2_g": 1.0 + nrm(ks[14], (L, D_MODEL), 0.05),
        "ln2_b": nrm(ks[15], (L, D_MODEL), 0.02),
    }


def reference(x, c, w_ada, b_ada, w_in, w_pool_mix, pool_scale, w_att_out, w_pool_out, w_o,
              ln1_g, ln1_b, w_ffn_in, w_ffn_out, ln2_g, ln2_b):
    B, S, D = x.shape
    slopes = alibi_slopes()
    for l in range(DEPTH):
        mod = jax.nn.silu(c) @ w_ada[l] + b_ada[l]
        shift1, scale1, gate1, shift2, scale2, gate2 = [m[:, None, :] for m in jnp.split(mod, N_MOD, axis=-1)]

        h = layer_norm(x) * (1.0 + scale1) + shift1
        proj = h @ w_in[l]
        qkv = proj[..., :QKV_WIDTH].reshape(B, S, N_GROUPS, 3, N_SLOTS, HEAD_DIM)
        pool_in = proj[..., QKV_WIDTH:QKV_WIDTH + POOL_WIDTH]
        gate_att = proj[..., QKV_WIDTH + POOL_WIDTH:QKV_WIDTH + POOL_WIDTH + D]
        gate_pool = proj[..., QKV_WIDTH + POOL_WIDTH + D:]

        outs, lses = [], []
        for g, (_, rate) in enumerate(DILATED_GROUPS):
            o, s = dilated_band_attention(qkv[:, :, g, 0], qkv[:, :, g, 1], qkv[:, :, g, 2], rate, slopes[g])
            outs.append(o)
            lses.append(s)
        wts = jax.nn.softmax(jnp.stack(lses, axis=0), axis=0)
        att = jnp.sum(wts[..., None].astype(x.dtype) * jnp.stack(outs, axis=0), axis=0)
        y_att = att.reshape(B, S, ATT_WIDTH) @ w_att_out[l]

        pu = pool_in.reshape(B, S, len(POOL_WINDOWS), POOL_GROUP)
        pooled = jnp.stack([centred_pool_minus_identity(pu[:, :, i], w) for i, w in enumerate(POOL_WINDOWS)], axis=2)
        pooled = jnp.einsum('bsgc,gcd->bsgd', pooled, w_pool_mix[l]).reshape(B, S, POOL_WIDTH) * pool_scale[l]
        y_pool = pooled @ w_pool_out[l]

        merged = jax.nn.sigmoid(gate_att) * y_att + jax.nn.sigmoid(gate_pool) * y_pool
        mix_out = merged @ w_o[l]
        x = layer_norm(ALPHA * x + (1.0 + gate1) * mix_out, ln1_g[l], ln1_b[l])

        h2 = layer_norm(x) * (1.0 + scale2) + shift2
        a_, b_ = jnp.split(h2 @ w_ffn_in[l], 2, axis=-1)
        ffn_out = (jax.nn.silu(a_) * b_) @ w_ffn_out[l]
        x = layer_norm(ALPHA * x + (1.0 + gate2) * ffn_out, ln2_g[l], ln2_b[l])
    return x
```

```python
import functools

import jax
import jax.numpy as jnp
import numpy as np
from jax import lax
from jax.experimental import pallas as pl
from jax.experimental.pallas import tpu as pltpu

F32 = jnp.float32
BF16 = jnp.bfloat16

D_MODEL = 1024
DEPTH = 2
HEAD_DIM = 64
N_SLOTS = 8
ATT_WIDTH = N_SLOTS * HEAD_DIM
RATES = (1, 4, 16)
N_GROUPS = len(RATES)
HALF_SPAN = 64
QKV_WIDTH = N_GROUPS * 3 * ATT_WIDTH
ALIBI_MAX = 8.0
POOL_WINDOWS = (2, 4, 8, 16)
POOL_GROUP = 128
POOL_WIDTH = len(POOL_WINDOWS) * POOL_GROUP
POOL_HALO = max(POOL_WINDOWS) // 2
IN_WIDTH = QKV_WIDTH + POOL_WIDTH + 2 * D_MODEL
D_FF = 2816
ALPHA = (2 * DEPTH) ** 0.25
LN_EPS = 1e-5
N_MOD = 6
MASK_VALUE = -1e30

Q_BLOCK = 128
K_BLOCK = Q_BLOCK + 2 * HALF_SPAN
MOD_ROWS = 8
V7X_VMEM_LIMIT = 56 * 1024 * 1024


def _dot(a, b):
    return jnp.dot(a, b, preferred_element_type=F32)


def _layer_norm(x):
    mu = jnp.mean(x, axis=-1, keepdims=True)
    xc = x - mu
    var = jnp.mean(xc * xc, axis=-1, keepdims=True)
    return xc * lax.rsqrt(var + LN_EPS)


def _resident(shape):
    zeros = (0,) * len(shape)
    return pl.BlockSpec(shape, lambda *_: zeros, pipeline_mode=pl.Buffered(1))


def _mod_kernel(c_ref, w_ref, b_ref, o_ref):
    c = c_ref[...]
    s = c * jax.nn.sigmoid(c)
    w = w_ref[0]
    s_hi = s.astype(BF16)
    s_lo = (s - s_hi.astype(F32)).astype(BF16)
    w_hi = w.astype(BF16)
    w_lo = (w - w_hi.astype(F32)).astype(BF16)
    acc = _dot(s_hi, w_hi) + _dot(s_hi, w_lo) + _dot(s_lo, w_hi)
    o_ref[0] = acc + b_ref[0]


def _modulation(c, w_ada, b_ada):
    batch, d = c.shape
    c_pad = jnp.pad(c, ((0, MOD_ROWS - batch), (0, 0)))
    b3 = b_ada.reshape(DEPTH * N_MOD, 1, d)
    mod = pl.pallas_call(
        _mod_kernel,
        grid=(DEPTH, N_MOD),
        in_specs=[
            pl.BlockSpec((MOD_ROWS, d), lambda l, k: (0, 0)),
            pl.BlockSpec((1, d, d), lambda l, k: (l, 0, k)),
            pl.BlockSpec((1, 1, d), lambda l, k: (l * N_MOD + k, 0, 0)),
        ],
        out_specs=pl.BlockSpec((1, MOD_ROWS, d), lambda l, k: (l * N_MOD + k, 0, 0)),
        out_shape=jax.ShapeDtypeStruct((DEPTH * N_MOD, MOD_ROWS, d), F32),
        compiler_params=pltpu.CompilerParams(dimension_semantics=("arbitrary", "arbitrary")),
        name="adaln_mod",
    )(c_pad, w_ada, b3)
    return mod.reshape(DEPTH * N_MOD * MOD_ROWS, 1, d)


def _mod_spec(layer, k, rows_per_batch_tiles):
    base = (layer * N_MOD + k) * MOD_ROWS
    return pl.BlockSpec((1, 1, D_MODEL), lambda i: (base + i // rows_per_batch_tiles, 0, 0))


def _inproj_kernel(x_ref, shift_ref, scale_ref, w_ref, *refs):
    qkv_refs = refs[: 3 * N_GROUPS]
    pool_ref, gates_ref, h_ref = refs[3 * N_GROUPS:]
    h = _layer_norm(x_ref[...]) * (1.0 + scale_ref[0]) + shift_ref[0]
    h_ref[...] = h.astype(BF16)
    for j, o_ref in enumerate(qkv_refs):
        r = _dot(h_ref[...], w_ref[:, j * ATT_WIDTH:(j + 1) * ATT_WIDTH])
        if j % 3 == 0:
            r = r * (HEAD_DIM ** -0.5)
        o_ref[...] = r.astype(BF16)
    pool_ref[...] = _dot(h_ref[...], w_ref[:, QKV_WIDTH:QKV_WIDTH + POOL_WIDTH])
    g0 = QKV_WIDTH + POOL_WIDTH
    for j in range(2 * D_MODEL // ATT_WIDTH):
        gates_ref[:, j * ATT_WIDTH:(j + 1) * ATT_WIDTH] = _dot(
            h_ref[...], w_ref[:, g0 + j * ATT_WIDTH:g0 + (j + 1) * ATT_WIDTH])


def _in_projection(xf, mod, w_in_bf, layer, seq, tm):
    tokens = xf.shape[0]
    tiles_per_seq = seq // tm
    row = lambda width: pl.BlockSpec((tm, width), lambda i: (i, 0))
    out_shape = [jax.ShapeDtypeStruct((tokens, ATT_WIDTH), BF16)] * (3 * N_GROUPS) + [
        jax.ShapeDtypeStruct((tokens, POOL_WIDTH), F32),
        jax.ShapeDtypeStruct((tokens, 2 * D_MODEL), F32)]
    out_specs = [row(ATT_WIDTH)] * (3 * N_GROUPS) + [row(POOL_WIDTH), row(2 * D_MODEL)]
    return pl.pallas_call(
        _inproj_kernel,
        grid=(tokens // tm,),
        in_specs=[row(D_MODEL), _mod_spec(layer, 0, tiles_per_seq), _mod_spec(layer, 1, tiles_per_seq),
                  _resident((D_MODEL, IN_WIDTH))],
        out_specs=out_specs,
        out_shape=out_shape,
        scratch_shapes=[pltpu.VMEM((tm, D_MODEL), BF16)],
        compiler_params=pltpu.CompilerParams(dimension_semantics=("arbitrary",),
                                             vmem_limit_bytes=V7X_VMEM_LIMIT),
        name="in_projection",
    )(xf, mod, mod, w_in_bf)


def _attention_bias(group, rate):
    n = N_GROUPS * N_SLOTS
    slopes = np.exp2(-ALIBI_MAX * np.arange(1, n + 1, dtype=np.float32) / n).reshape(N_GROUPS, N_SLOTS)
    rel = np.arange(K_BLOCK)[None, :] - HALF_SPAN - np.arange(Q_BLOCK)[:, None]
    band = np.abs(rel) <= HALF_SPAN
    dist = (rate * np.abs(rel)).astype(np.float32)
    col = np.arange(K_BLOCK)[None, :]
    edge = [col >= HALF_SPAN, col >= 0, col < K_BLOCK - HALF_SPAN]
    out = np.empty((3, N_SLOTS // 2, 2 * Q_BLOCK, K_BLOCK), np.float32)
    for e in range(3):
        for h in range(N_SLOTS):
            b = np.where(band & edge[e], -slopes[group, h] * dist, np.float32(MASK_VALUE))
            out[e, h // 2, (h % 2) * Q_BLOCK:(h % 2 + 1) * Q_BLOCK] = b
    return jnp.asarray(out)


def _attn_kernel(*refs, n_ph, lt, has_halo, n_blocks_total):
    if has_halo:
        q_ref, k_ref, kp_ref, kn_ref, v_ref, vp_ref, vn_ref, bias_ref, o_ref, lse_ref, kwin, vwin = refs
    else:
        q_ref, k_ref, v_ref, bias_ref, o_ref, lse_ref, kwin, vwin = refs
    t = pl.program_id(2)
    nblk = lt // Q_BLOCK
    low = lax.broadcasted_iota(jnp.int32, (Q_BLOCK, 2 * HEAD_DIM), 1) < HEAD_DIM

    for ph in range(n_ph):
        c0 = ph * ATT_WIDTH
        kwin[HALF_SPAN:HALF_SPAN + lt, :] = k_ref[0, :, c0:c0 + ATT_WIDTH]
        vwin[HALF_SPAN:HALF_SPAN + lt, :] = v_ref[0, :, c0:c0 + ATT_WIDTH]
        if has_halo:
            kwin[0:HALF_SPAN, :] = kp_ref[0, :, c0:c0 + ATT_WIDTH]
            vwin[0:HALF_SPAN, :] = vp_ref[0, :, c0:c0 + ATT_WIDTH]
            kwin[HALF_SPAN + lt:, :] = kn_ref[0, :, c0:c0 + ATT_WIDTH]
            vwin[HALF_SPAN + lt:, :] = vn_ref[0, :, c0:c0 + ATT_WIDTH]
        else:
            zeros = jnp.zeros((HALF_SPAN, ATT_WIDTH), BF16)
            kwin[0:HALF_SPAN, :] = zeros
            vwin[0:HALF_SPAN, :] = zeros
            kwin[HALF_SPAN + lt:, :] = zeros
            vwin[HALF_SPAN + lt:, :] = zeros

        for pair in range(N_SLOTS // 2):
            l0 = pair * 2 * HEAD_DIM

            def block(j, carry, c0=c0, l0=l0, pair=pair):
                a = pl.multiple_of(j * Q_BLOCK, Q_BLOCK)
                q2 = q_ref[0, pl.ds(a, Q_BLOCK), c0 + l0:c0 + l0 + 2 * HEAD_DIM]
                zero = jnp.zeros_like(q2)
                qq = jnp.concatenate([jnp.where(low, q2, zero), jnp.where(low, zero, q2)], axis=0)
                kk = kwin[pl.ds(a, K_BLOCK), l0:l0 + 2 * HEAD_DIM]
                vv = vwin[pl.ds(a, K_BLOCK), l0:l0 + 2 * HEAD_DIM]
                s = lax.dot_general(qq, kk, (((1,), (1,)), ((), ())), preferred_element_type=F32)
                gblk = t * nblk + j
                variant = jnp.where(gblk == 0, 0, jnp.where(gblk == n_blocks_total - 1, 2, 1))
                s = s + bias_ref[variant, pair]
                m = jnp.max(s, axis=-1, keepdims=True)
                p = jnp.exp(s - m)
                denom = jnp.sum(p, axis=-1, keepdims=True)
                pv = _dot(p.astype(BF16), vv)
                pv = pv / denom
                lse = m + jnp.log(denom)
                o_ref[0, pl.ds(a, Q_BLOCK), c0 + l0:c0 + l0 + 2 * HEAD_DIM] = jnp.where(
                    low, pv[:Q_BLOCK], pv[Q_BLOCK:])
                lse_ref[0, pl.ds(a, Q_BLOCK), c0 + l0:c0 + l0 + 2 * HEAD_DIM] = jnp.where(
                    low, jnp.broadcast_to(lse[:Q_BLOCK], (Q_BLOCK, 2 * HEAD_DIM)),
                    jnp.broadcast_to(lse[Q_BLOCK:], (Q_BLOCK, 2 * HEAD_DIM)))
                return carry

            lax.fori_loop(0, nblk, block, 0)


def _attention_group(q, k, v, group, batch, seq, lt, n_ph):
    rate = RATES[group]
    length = seq // rate
    width = rate * ATT_WIDTH
    view = lambda a: a.reshape(batch, length, width)
    n_tiles = length // lt
    has_halo = n_tiles > 1
    halo_blocks = lt // HALF_SPAN
    main = pl.BlockSpec((1, lt, n_ph * ATT_WIDTH), lambda b, g, t: (b, t, g))
    prev = pl.BlockSpec((1, HALF_SPAN, n_ph * ATT_WIDTH),
                        lambda b, g, t: (b, jnp.maximum(t * halo_blocks - 1, 0), g))
    nxt = pl.BlockSpec((1, HALF_SPAN, n_ph * ATT_WIDTH),
                       lambda b, g, t: (b, jnp.minimum((t + 1) * halo_blocks, length // HALF_SPAN - 1), g))
    bias = _attention_bias(group, rate)
    bias_spec = _resident(bias.shape)
    if has_halo:
        in_specs = [main, main, prev, nxt, main, prev, nxt, bias_spec]
        args = (view(q), view(k), view(k), view(k), view(v), view(v), view(v), bias)
    else:
        in_specs = [main, main, main, bias_spec]
        args = (view(q), view(k), view(v), bias)
    out, lse = pl.pallas_call(
        functools.partial(_attn_kernel, n_ph=n_ph, lt=lt, has_halo=has_halo,
                          n_blocks_total=length // Q_BLOCK),
        grid=(batch, rate // n_ph, n_tiles),
        in_specs=in_specs,
        out_specs=[main, main],
        out_shape=[jax.ShapeDtypeStruct((batch, length, width), F32)] * 2,
        scratch_shapes=[pltpu.VMEM((lt + 2 * HALF_SPAN, ATT_WIDTH), BF16)] * 2,
        compiler_params=pltpu.CompilerParams(dimension_semantics=("arbitrary",) * 3,
                                             vmem_limit_bytes=V7X_VMEM_LIMIT),
        name=f"dilated_attention_g{group}",
    )(*args)
    return out.reshape(batch * seq, ATT_WIDTH), lse.reshape(batch * seq, ATT_WIDTH)


def _mix_kernel(o0_ref, l0_ref, o1_ref, l1_ref, o2_ref, l2_ref, pm_ref, pp_ref, pn_ref, gates_ref,
                x_ref, gate_ref, wao_ref, wpm_ref, wpo_ref, wo_ref, ps_ref, lng_ref, lnb_ref,
                out_ref, ext_ref, *, tm, seq):
    tiles_per_seq = seq // tm
    ti = pl.program_id(0) % tiles_per_seq

    l0, l1, l2 = l0_ref[...], l1_ref[...], l2_ref[...]
    top = jnp.maximum(jnp.maximum(l0, l1), l2)
    e0, e1, e2 = jnp.exp(l0 - top), jnp.exp(l1 - top), jnp.exp(l2 - top)
    att = (e0 * o0_ref[...] + e1 * o1_ref[...] + e2 * o2_ref[...]) / (e0 + e1 + e2)
    y_att = _dot(att.astype(BF16), wao_ref[...])

    halo_zero = jnp.zeros((POOL_HALO, POOL_WIDTH), F32)
    ext_ref[0:POOL_HALO, :] = jnp.where(ti == 0, halo_zero, pp_ref[...])
    ext_ref[POOL_HALO:POOL_HALO + tm, :] = pm_ref[...]
    ext_ref[POOL_HALO + tm:, :] = jnp.where(ti == tiles_per_seq - 1, halo_zero, pn_ref[...])
    pos = ti * tm + lax.broadcasted_iota(jnp.int32, (tm, POOL_GROUP), 0)
    mixed = []
    for g, window in enumerate(POOL_WINDOWS):
        half = window // 2
        cols = slice(g * POOL_GROUP, (g + 1) * POOL_GROUP)
        total = ext_ref[POOL_HALO - half:POOL_HALO - half + tm, cols]
        for s in range(-half + 1, half):
            total = total + ext_ref[POOL_HALO + s:POOL_HALO + s + tm, cols]
        count = (jnp.minimum(pos + half, seq) - jnp.maximum(pos - half, 0)).astype(F32)
        pooled = total / count - pm_ref[:, cols]
        mixed.append(_dot(pooled.astype(BF16), wpm_ref[g]) * ps_ref[:, cols])
    y_pool = _dot(jnp.concatenate(mixed, axis=1).astype(BF16), wpo_ref[...])

    merged = (jax.nn.sigmoid(gates_ref[:, :D_MODEL]) * y_att
              + jax.nn.sigmoid(gates_ref[:, D_MODEL:]) * y_pool)
    mix_out = _dot(merged.astype(BF16), wo_ref[...])
    z = ALPHA * x_ref[...] + (1.0 + gate_ref[0]) * mix_out
    out_ref[...] = _layer_norm(z) * lng_ref[...] + lnb_ref[...]


def _mix_epilogue(att_parts, pool_in, gates, xf, mod, layer, seq, tm, wao, wpm, wpo, wo, pool_scale,
                  ln_g, ln_b):
    tokens = xf.shape[0]
    tiles_per_seq = seq // tm
    halo_per_tile = tm // POOL_HALO
    row = lambda width: pl.BlockSpec((tm, width), lambda i: (i, 0))
    prev = pl.BlockSpec((POOL_HALO, POOL_WIDTH), lambda i: (jnp.maximum(i * halo_per_tile - 1, 0), 0))
    nxt = pl.BlockSpec((POOL_HALO, POOL_WIDTH),
                       lambda i: (jnp.minimum((i + 1) * halo_per_tile, tokens // POOL_HALO - 1), 0))
    flat = [a for part in att_parts for a in part]
    return pl.pallas_call(
        functools.partial(_mix_kernel, tm=tm, seq=seq),
        grid=(tokens // tm,),
        in_specs=[row(ATT_WIDTH)] * 6 + [row(POOL_WIDTH), prev, nxt, row(2 * D_MODEL), row(D_MODEL),
                                         _mod_spec(layer, 2, tiles_per_seq),
                                         _resident(wao.shape), _resident(wpm.shape), _resident(wpo.shape),
                                         _resident(wo.shape), _resident((1, POOL_WIDTH)),
                                         _resident((1, D_MODEL)), _resident((1, D_MODEL))],
        out_specs=row(D_MODEL),
        out_shape=jax.ShapeDtypeStruct((tokens, D_MODEL), F32),
        scratch_shapes=[pltpu.VMEM((tm + 2 * POOL_HALO, POOL_WIDTH), F32)],
        compiler_params=pltpu.CompilerParams(dimension_semantics=("arbitrary",),
                                             vmem_limit_bytes=V7X_VMEM_LIMIT),
        name="mix_epilogue",
    )(*flat, pool_in, pool_in, pool_in, gates, xf, mod, wao, wpm, wpo, wo,
      pool_scale.reshape(1, POOL_WIDTH), ln_g.reshape(1, D_MODEL), ln_b.reshape(1, D_MODEL))


FFN_CHUNK = 256


def _ffn_kernel(x_ref, shift_ref, scale_ref, gate_ref, w1_ref, w2_ref, lng_ref, lnb_ref, out_ref,
                h_ref, act_ref):
    x = x_ref[...]
    h_ref[...] = (_layer_norm(x) * (1.0 + scale_ref[0]) + shift_ref[0]).astype(BF16)
    for c in range(D_FF // FFN_CHUNK):
        lo = c * FFN_CHUNK
        a = _dot(h_ref[...], w1_ref[:, lo:lo + FFN_CHUNK])
        b = _dot(h_ref[...], w1_ref[:, D_FF + lo:D_FF + lo + FFN_CHUNK])
        act_ref[:, lo:lo + FFN_CHUNK] = (a * jax.nn.sigmoid(a) * b).astype(BF16)
    ffn_out = _dot(act_ref[...], w2_ref[...])
    z = ALPHA * x + (1.0 + gate_ref[0]) * ffn_out
    out_ref[...] = _layer_norm(z) * lng_ref[...] + lnb_ref[...]


def _ffn(xf, mod, layer, seq, tm, w1, w2, ln_g, ln_b):
    tokens = xf.shape[0]
    tiles_per_seq = seq // tm
    row = pl.BlockSpec((tm, D_MODEL), lambda i: (i, 0))
    return pl.pallas_call(
        _ffn_kernel,
        grid=(tokens // tm,),
        in_specs=[row, _mod_spec(layer, 3, tiles_per_seq), _mod_spec(layer, 4, tiles_per_seq),
                  _mod_spec(layer, 5, tiles_per_seq), _resident(w1.shape), _resident(w2.shape),
                  _resident((1, D_MODEL)), _resident((1, D_MODEL))],
        out_specs=row,
        out_shape=jax.ShapeDtypeStruct((tokens, D_MODEL), F32),
        scratch_shapes=[pltpu.VMEM((tm, D_MODEL), BF16), pltpu.VMEM((tm, D_FF), BF16)],
        compiler_params=pltpu.CompilerParams(dimension_semantics=("arbitrary",),
                                             vmem_limit_bytes=V7X_VMEM_LIMIT),
        name="swiglu_ffn",
    )(xf, mod, mod, mod, w1, w2, ln_g.reshape(1, D_MODEL), ln_b.reshape(1, D_MODEL))


def _attention_tiling(seq, rate):
    length = seq // rate
    lt = min(length, 2048)
    n_ph = max(1, min(rate, 2048 // lt))
    return lt, n_ph


def kernel(x, c, w_ada, b_ada, w_in, w_pool_mix, pool_scale, w_att_out, w_pool_out, w_o, ln1_g, ln1_b,
           w_ffn_in, w_ffn_out, ln2_g, ln2_b):
    batch, seq, d = x.shape
    assert d == D_MODEL and seq % (RATES[-1] * 2 * Q_BLOCK) == 0
    tm = 512
    mod = _modulation(c, w_ada, b_ada)
    xf = x.reshape(batch * seq, d)
    for layer in range(DEPTH):
        outs = _in_projection(xf, mod, w_in[layer].astype(BF16), layer, seq, tm)
        qkv, pool_in, gates = outs[:3 * N_GROUPS], outs[3 * N_GROUPS], outs[3 * N_GROUPS + 1]
        att_parts = []
        for g, rate in enumerate(RATES):
            lt, n_ph = _attention_tiling(seq, rate)
            att_parts.append(_attention_group(qkv[3 * g], qkv[3 * g + 1], qkv[3 * g + 2], g, batch, seq,
                                              lt, n_ph))
        xf = _mix_epilogue(att_parts, pool_in, gates, xf, mod, layer, seq, tm,
                           w_att_out[layer].astype(BF16), w_pool_mix[layer].astype(BF16),
                           w_pool_out[layer].astype(BF16), w_o[layer].astype(BF16), pool_scale[layer],
                           ln1_g[layer], ln1_b[layer])
        xf = _ffn(xf, mod, layer, seq, tm, w_ffn_in[layer].astype(BF16), w_ffn_out[layer].astype(BF16),
                  ln2_g[layer], ln2_b[layer])
    return xf.reshape(batch, seq, d)
```

```python
import functools

import jax
import jax.numpy as jnp
import numpy as np
from jax import lax
from jax.experimental import pallas as pl
from jax.experimental.pallas import tpu as pltpu

F32 = jnp.float32
BF16 = jnp.bfloat16

D_MODEL = 1024
DEPTH = 2
HEAD_DIM = 64
N_SLOTS = 8
ATT_WIDTH = N_SLOTS * HEAD_DIM
RATES = (1, 4, 16)
N_GROUPS = len(RATES)
HALF_SPAN = 64
QKV_WIDTH = N_GROUPS * 3 * ATT_WIDTH
ALIBI_MAX = 8.0
POOL_WINDOWS = (2, 4, 8, 16)
POOL_GROUP = 128
POOL_WIDTH = len(POOL_WINDOWS) * POOL_GROUP
POOL_HALO = max(POOL_WINDOWS) // 2
IN_WIDTH = QKV_WIDTH + POOL_WIDTH + 2 * D_MODEL
D_FF = 2816
ALPHA = (2 * DEPTH) ** 0.25
LN_EPS = 1e-5
N_MOD = 6
MASK_VALUE = -1e30

LANES = 128
N_PAIRS = ATT_WIDTH // LANES
Q_BLOCK = 128
K_BLOCK = Q_BLOCK + 2 * HALF_SPAN
SCORE_LEAD = 4
MOD_ROWS = 8
V7X_VMEM_LIMIT = 56 * 1024 * 1024


def _dot(a, b):
    return jnp.dot(a, b, preferred_element_type=F32)


def _layer_norm(x):
    mu = jnp.mean(x, axis=-1, keepdims=True)
    xc = x - mu
    var = jnp.mean(xc * xc, axis=-1, keepdims=True)
    return xc * lax.rsqrt(var + LN_EPS)


def _resident(shape):
    zeros = (0,) * len(shape)
    return pl.BlockSpec(shape, lambda *_: zeros, pipeline_mode=pl.Buffered(1))


def _mod_kernel(c_ref, w_ref, b_ref, o_ref):
    c = c_ref[...]
    s = c * jax.nn.sigmoid(c)
    w = w_ref[0]
    s_hi = s.astype(BF16)
    s_lo = (s - s_hi.astype(F32)).astype(BF16)
    w_hi = w.astype(BF16)
    w_lo = (w - w_hi.astype(F32)).astype(BF16)
    acc = _dot(s_hi, w_hi) + _dot(s_hi, w_lo) + _dot(s_lo, w_hi)
    o_ref[0] = acc + b_ref[0]


def _modulation(c, w_ada, b_ada):
    batch, d = c.shape
    c_pad = jnp.pad(c, ((0, MOD_ROWS - batch), (0, 0)))
    b3 = b_ada.reshape(DEPTH * N_MOD, 1, d)
    mod = pl.pallas_call(
        _mod_kernel,
        grid=(DEPTH, N_MOD),
        in_specs=[
            pl.BlockSpec((MOD_ROWS, d), lambda l, k: (0, 0)),
            pl.BlockSpec((1, d, d), lambda l, k: (l, 0, k)),
            pl.BlockSpec((1, 1, d), lambda l, k: (l * N_MOD + k, 0, 0)),
        ],
        out_specs=pl.BlockSpec((1, MOD_ROWS, d), lambda l, k: (l * N_MOD + k, 0, 0)),
        out_shape=jax.ShapeDtypeStruct((DEPTH * N_MOD, MOD_ROWS, d), F32),
        compiler_params=pltpu.CompilerParams(dimension_semantics=("arbitrary", "arbitrary")),
        name="adaln_mod",
    )(c_pad, w_ada, b3)
    return mod.reshape(DEPTH * N_MOD * MOD_ROWS, 1, d)


def _mod_spec(layer, k, rows_per_batch_tiles):
    base = (layer * N_MOD + k) * MOD_ROWS
    return pl.BlockSpec((1, 1, D_MODEL), lambda i: (base + i // rows_per_batch_tiles, 0, 0))


def _inproj_kernel(x_ref, shift_ref, scale_ref, w_ref, *refs, tm):
    qkv_refs = refs[: 3 * N_GROUPS]
    pool_ref, gates_ref, hs_ref = refs[3 * N_GROUPS:3 * N_GROUPS + 3]
    h_refs = refs[3 * N_GROUPS + 3:]
    h = _layer_norm(x_ref[...]) * (1.0 + scale_ref[0]) + shift_ref[0]
    h_refs[0][...] = h.astype(BF16)
    n_slabs = D_MODEL // LANES
    for s in range(n_slabs):
        hs_ref[s] = h[:, s * LANES:(s + 1) * LANES]
    for g in range(1, N_GROUPS):
        rate = RATES[g]
        n = tm // rate
        for p in range(rate):
            for s in range(n_slabs):
                h_refs[g][p * n:(p + 1) * n, s * LANES:(s + 1) * LANES] = (
                    hs_ref[s, pl.ds(p, n, stride=rate), :].astype(BF16))
    for j, o_ref in enumerate(qkv_refs):
        g = j // 3
        rate = RATES[g]
        n = tm // rate
        r = _dot(h_refs[g][...], w_ref[:, j * ATT_WIDTH:(j + 1) * ATT_WIDTH])
        if j % 3 == 0:
            r = r * (HEAD_DIM ** -0.5)
        r = r.astype(BF16)
        for p in range(rate):
            o_ref[0, p] = r[p * n:(p + 1) * n]
    pool_ref[...] = _dot(h_refs[0][...], w_ref[:, QKV_WIDTH:QKV_WIDTH + POOL_WIDTH])
    g0 = QKV_WIDTH + POOL_WIDTH
    for j in range(2 * D_MODEL // ATT_WIDTH):
        gates_ref[:, j * ATT_WIDTH:(j + 1) * ATT_WIDTH] = _dot(
            h_refs[0][...], w_ref[:, g0 + j * ATT_WIDTH:g0 + (j + 1) * ATT_WIDTH])


def _in_projection(xf, mod, w_in_bf, layer, batch, seq, tm):
    tokens = xf.shape[0]
    tiles_per_seq = seq // tm
    row = lambda width: pl.BlockSpec((tm, width), lambda i: (i, 0))
    phase_major = lambda rate: pl.BlockSpec(
        (1, rate, tm // rate, ATT_WIDTH), lambda i: (i // tiles_per_seq, 0, i % tiles_per_seq, 0))
    out_shape, out_specs = [], []
    for rate in RATES:
        out_shape += [jax.ShapeDtypeStruct((batch, rate, seq // rate, ATT_WIDTH), BF16)] * 3
        out_specs += [phase_major(rate)] * 3
    out_shape += [jax.ShapeDtypeStruct((tokens, POOL_WIDTH), F32),
                  jax.ShapeDtypeStruct((tokens, 2 * D_MODEL), F32)]
    out_specs += [row(POOL_WIDTH), row(2 * D_MODEL)]
    return pl.pallas_call(
        functools.partial(_inproj_kernel, tm=tm),
        grid=(tokens // tm,),
        in_specs=[row(D_MODEL), _mod_spec(layer, 0, tiles_per_seq), _mod_spec(layer, 1, tiles_per_seq),
                  _resident((D_MODEL, IN_WIDTH))],
        out_specs=out_specs,
        out_shape=out_shape,
        scratch_shapes=[pltpu.VMEM((D_MODEL // LANES, tm, LANES), F32)]
        + [pltpu.VMEM((tm, D_MODEL), BF16)] * N_GROUPS,
        compiler_params=pltpu.CompilerParams(dimension_semantics=("arbitrary",),
                                             vmem_limit_bytes=V7X_VMEM_LIMIT),
        name="in_projection",
    )(xf, mod, mod, w_in_bf)


def _attention_bias(group, n_q):
    rate = RATES[group]
    n = N_GROUPS * N_SLOTS
    slopes = np.exp2(-ALIBI_MAX * np.arange(1, n + 1, dtype=np.float32) / n).reshape(N_GROUPS, N_SLOTS)
    col = np.arange(K_BLOCK)[None, :]
    rel = col - HALF_SPAN - np.arange(n_q)[:, None]
    band = np.abs(rel) <= HALF_SPAN
    dist = (rate * np.abs(rel)).astype(np.float32)
    edge = [col >= HALF_SPAN, col >= 0, col < n_q + HALF_SPAN]
    out = np.empty((3, N_PAIRS, 2 * n_q, K_BLOCK), np.float32)
    for e in range(3):
        for h in range(N_SLOTS):
            b = np.where(band & edge[e], -slopes[group, h] * dist, np.float32(MASK_VALUE))
            out[e, h // 2, (h % 2) * n_q:(h % 2 + 1) * n_q] = b
    return jnp.asarray(out)


def _scores(q2, kk, bias, low):
    zero = jnp.zeros_like(q2)
    qq = jnp.concatenate([jnp.where(low, q2, zero), jnp.where(low, zero, q2)], axis=0)
    return lax.dot_general(qq, kk, (((1,), (1,)), ((), ())), preferred_element_type=F32) + bias


def _softmax_pv(s, vv, low):
    n = s.shape[0] // 2
    m = jnp.max(s, axis=-1, keepdims=True)
    p = jnp.exp(s - m)
    denom = jnp.sum(p, axis=-1, keepdims=True)
    pv = _dot(p.astype(BF16), vv)
    wide = lambda c: jnp.where(low, jnp.broadcast_to(c[:n], (n, LANES)), jnp.broadcast_to(c[n:], (n, LANES)))
    return jnp.where(low, pv[:n], pv[n:]), wide(m), wide(denom)


def _attn_kernel(*refs, tile, tiles_per_seq):
    group_refs = [refs[8 * g:8 * g + 8] for g in range(N_GROUPS)]
    o_ref, acc_ref, max_ref, sum_ref = refs[8 * N_GROUPS:]
    t = pl.program_id(1)
    first_tile = t == 0
    last_tile = t == tiles_per_seq - 1

    def window(main_ref, prev_ref, next_ref, p, a, cols, lt):
        start, end = a - HALF_SPAN, a - HALF_SPAN + K_BLOCK
        pieces = []
        if start < 0:
            pieces.append(prev_ref[0, p, :, cols])
        pieces.append(main_ref[0, p, max(start, 0):min(end, lt), cols])
        if end > lt:
            pieces.append(next_ref[0, p, 0:min(end - lt, HALF_SPAN), cols])
            if end - lt > HALF_SPAN:
                pieces.append(jnp.zeros((end - lt - HALF_SPAN, LANES), BF16))
        return pieces[0] if len(pieces) == 1 else jnp.concatenate(pieces, axis=0)

    chains = []
    for g, rate in enumerate(RATES):
        lt = tile // rate
        n_q = min(Q_BLOCK, lt)
        n_blocks = lt // n_q
        for p in range(rate):
            for j in range(n_blocks):
                for pair in range(N_PAIRS):
                    chains.append((g, rate, lt, n_q, n_blocks, p, j, pair))

    def start(chain):
        g, rate, lt, n_q, n_blocks, p, j, pair = chain
        q_ref, k_ref, kp_ref, kn_ref, _, _, _, bias_ref = group_refs[g]
        a = j * n_q
        cols = slice(pair * LANES, (pair + 1) * LANES)
        low = lax.broadcasted_iota(jnp.int32, (n_q, LANES), 1) < HEAD_DIM
        if j == 0 and j == n_blocks - 1:
            variant = jnp.where(first_tile, 0, jnp.where(last_tile, 2, 1))
        elif j == 0:
            variant = jnp.where(first_tile, 0, 1)
        elif j == n_blocks - 1:
            variant = jnp.where(last_tile, 2, 1)
        else:
            variant = 1
        kk = window(k_ref, kp_ref, kn_ref, p, a, cols, lt)
        return _scores(q_ref[0, p, a:a + n_q, cols], kk, bias_ref[variant, pair], low)

    def finish(chain, s):
        g, rate, lt, n_q, n_blocks, p, j, pair = chain
        _, _, _, _, v_ref, vp_ref, vn_ref, _ = group_refs[g]
        a = j * n_q
        cols = slice(pair * LANES, (pair + 1) * LANES)
        low = lax.broadcasted_iota(jnp.int32, (n_q, LANES), 1) < HEAD_DIM
        u, m, d = _softmax_pv(s, window(v_ref, vp_ref, vn_ref, p, a, cols, lt), low)
        rows = pl.ds(a, n_q) if rate == 1 else pl.ds(a * rate + p, n_q, stride=rate)
        if g == 0:
            acc_ref[pair, rows, :] = u
            max_ref[pair, rows, :] = m
            sum_ref[pair, rows, :] = d
        else:
            m_old = max_ref[pair, rows, :]
            m_new = jnp.maximum(m_old, m)
            w_old = jnp.exp(m_old - m_new)
            w_new = jnp.exp(m - m_new)
            acc_ref[pair, rows, :] = acc_ref[pair, rows, :] * w_old + u * w_new
            sum_ref[pair, rows, :] = sum_ref[pair, rows, :] * w_old + d * w_new
            max_ref[pair, rows, :] = m_new

    pending = {}
    for i in range(len(chains) + SCORE_LEAD):
        if i < len(chains):
            pending[i] = start(chains[i])
        if i >= SCORE_LEAD:
            finish(chains[i - SCORE_LEAD], pending.pop(i - SCORE_LEAD))

    for pair in range(N_PAIRS):
        o_ref[:, pair * LANES:(pair + 1) * LANES] = (acc_ref[pair] / sum_ref[pair]).astype(BF16)


def _attention(qkv, batch, seq, tile):
    tiles_per_seq = seq // tile
    in_specs, args = [], []
    for g, rate in enumerate(RATES):
        lt = tile // rate
        length = seq // rate
        halo_per_tile = lt // HALF_SPAN
        main = pl.BlockSpec((1, rate, lt, ATT_WIDTH), lambda b, t: (b, 0, t, 0))
        prev = pl.BlockSpec((1, rate, HALF_SPAN, ATT_WIDTH),
                            lambda b, t, h=halo_per_tile: (b, 0, jnp.maximum(t * h - 1, 0), 0))
        nxt = pl.BlockSpec((1, rate, HALF_SPAN, ATT_WIDTH),
                           lambda b, t, h=halo_per_tile, last=length // HALF_SPAN - 1:
                           (b, 0, jnp.minimum((t + 1) * h, last), 0))
        q, k, v = qkv[3 * g:3 * g + 3]
        bias = _attention_bias(g, min(Q_BLOCK, lt))
        in_specs += [main, main, prev, nxt, main, prev, nxt, _resident(bias.shape)]
        args += [q, k, k, k, v, v, v, bias]
    return pl.pallas_call(
        functools.partial(_attn_kernel, tile=tile, tiles_per_seq=tiles_per_seq),
        grid=(batch, tiles_per_seq),
        in_specs=in_specs,
        out_specs=pl.BlockSpec((tile, ATT_WIDTH), lambda b, t: (b * tiles_per_seq + t, 0)),
        out_shape=jax.ShapeDtypeStruct((batch * seq, ATT_WIDTH), BF16),
        scratch_shapes=[pltpu.VMEM((N_PAIRS, tile, LANES), F32)] * 3,
        compiler_params=pltpu.CompilerParams(dimension_semantics=("arbitrary", "arbitrary"),
                                             vmem_limit_bytes=V7X_VMEM_LIMIT),
        name="dilated_attention",
    )(*args)


def _mix_kernel(att_ref, pm_ref, pp_ref, pn_ref, gates_ref, x_ref, gate_ref, wao_ref, wpm_ref, wpo_ref,
                wo_ref, ps_ref, lng_ref, lnb_ref, out_ref, ext_ref, *, tm, seq):
    tiles_per_seq = seq // tm
    ti = pl.program_id(0) % tiles_per_seq
    y_att = _dot(att_ref[...], wao_ref[...])

    halo_zero = jnp.zeros((POOL_HALO, POOL_WIDTH), F32)
    ext_ref[0:POOL_HALO, :] = jnp.where(ti == 0, halo_zero, pp_ref[...])
    ext_ref[POOL_HALO:POOL_HALO + tm, :] = pm_ref[...]
    ext_ref[POOL_HALO + tm:, :] = jnp.where(ti == tiles_per_seq - 1, halo_zero, pn_ref[...])
    pos = ti * tm + lax.broadcasted_iota(jnp.int32, (tm, POOL_GROUP), 0)
    mixed = []
    for g, window in enumerate(POOL_WINDOWS):
        half = window // 2
        cols = slice(g * POOL_GROUP, (g + 1) * POOL_GROUP)
        total = ext_ref[POOL_HALO - half:POOL_HALO - half + tm, cols]
        for s in range(-half + 1, half):
            total = total + ext_ref[POOL_HALO + s:POOL_HALO + s + tm, cols]
        count = (jnp.minimum(pos + half, seq) - jnp.maximum(pos - half, 0)).astype(F32)
        pooled = total / count - pm_ref[:, cols]
        mixed.append(_dot(pooled.astype(BF16), wpm_ref[g]) * ps_ref[:, cols])
    y_pool = _dot(jnp.concatenate(mixed, axis=1).astype(BF16), wpo_ref[...])

    merged = (jax.nn.sigmoid(gates_ref[:, :D_MODEL]) * y_att
              + jax.nn.sigmoid(gates_ref[:, D_MODEL:]) * y_pool)
    mix_out = _dot(merged.astype(BF16), wo_ref[...])
    z = ALPHA * x_ref[...] + (1.0 + gate_ref[0]) * mix_out
    out_ref[...] = _layer_norm(z) * lng_ref[...] + lnb_ref[...]


def _mix_epilogue(att, pool_in, gates, xf, mod, layer, seq, tm, wao, wpm, wpo, wo, pool_scale, ln_g, ln_b):
    tokens = xf.shape[0]
    tiles_per_seq = seq // tm
    halo_per_tile = tm // POOL_HALO
    row = lambda width: pl.BlockSpec((tm, width), lambda i: (i, 0))
    prev = pl.BlockSpec((POOL_HALO, POOL_WIDTH), lambda i: (jnp.maximum(i * halo_per_tile - 1, 0), 0))
    nxt = pl.BlockSpec((POOL_HALO, POOL_WIDTH),
                       lambda i: (jnp.minimum((i + 1) * halo_per_tile, tokens // POOL_HALO - 1), 0))
    return pl.pallas_call(
        functools.partial(_mix_kernel, tm=tm, seq=seq),
        grid=(tokens // tm,),
        in_specs=[row(ATT_WIDTH), row(POOL_WIDTH), prev, nxt, row(2 * D_MODEL), row(D_MODEL),
                  _mod_spec(layer, 2, tiles_per_seq),
                  _resident(wao.shape), _resident(wpm.shape), _resident(wpo.shape),
                  _resident(wo.shape), _resident((1, POOL_WIDTH)),
                  _resident((1, D_MODEL)), _resident((1, D_MODEL))],
        out_specs=row(D_MODEL),
        out_shape=jax.ShapeDtypeStruct((tokens, D_MODEL), F32),
        scratch_shapes=[pltpu.VMEM((tm + 2 * POOL_HALO, POOL_WIDTH), F32)],
        compiler_params=pltpu.CompilerParams(dimension_semantics=("arbitrary",),
                                             vmem_limit_bytes=V7X_VMEM_LIMIT),
        name="mix_epilogue",
    )(att, pool_in, pool_in, pool_in, gates, xf, mod, wao, wpm, wpo, wo,
      pool_scale.reshape(1, POOL_WIDTH), ln_g.reshape(1, D_MODEL), ln_b.reshape(1, D_MODEL))


FFN_CHUNK = 256


def _ffn_kernel(x_ref, shift_ref, scale_ref, gate_ref, w1_ref, w2_ref, lng_ref, lnb_ref, out_ref,
                h_ref, act_ref):
    x = x_ref[...]
    h_ref[...] = (_layer_norm(x) * (1.0 + scale_ref[0]) + shift_ref[0]).astype(BF16)
    for c in range(D_FF // FFN_CHUNK):
        lo = c * FFN_CHUNK
        a = _dot(h_ref[...], w1_ref[:, lo:lo + FFN_CHUNK])
        b = _dot(h_ref[...], w1_ref[:, D_FF + lo:D_FF + lo + FFN_CHUNK])
        act_ref[:, lo:lo + FFN_CHUNK] = (a * jax.nn.sigmoid(a) * b).astype(BF16)
    ffn_out = _dot(act_ref[...], w2_ref[...])
    z = ALPHA * x + (1.0 + gate_ref[0]) * ffn_out
    out_ref[...] = _layer_norm(z) * lng_ref[...] + lnb_ref[...]


def _ffn(xf, mod, layer, seq, tm, w1, w2, ln_g, ln_b):
    tokens = xf.shape[0]
    tiles_per_seq = seq // tm
    row = pl.BlockSpec((tm, D_MODEL), lambda i: (i, 0))
    return pl.pallas_call(
        _ffn_kernel,
        grid=(tokens // tm,),
        in_specs=[row, _mod_spec(layer, 3, tiles_per_seq), _mod_spec(layer, 4, tiles_per_seq),
                  _mod_spec(layer, 5, tiles_per_seq), _resident(w1.shape), _resident(w2.shape),
                  _resident((1, D_MODEL)), _resident((1, D_MODEL))],
        out_specs=row,
        out_shape=jax.ShapeDtypeStruct((tokens, D_MODEL), F32),
        scratch_shapes=[pltpu.VMEM((tm, D_MODEL), BF16), pltpu.VMEM((tm, D_FF), BF16)],
        compiler_params=pltpu.CompilerParams(dimension_semantics=("arbitrary",),
                                             vmem_limit_bytes=V7X_VMEM_LIMIT),
        name="swiglu_ffn",
    )(xf, mod, mod, mod, w1, w2, ln_g.reshape(1, D_MODEL), ln_b.reshape(1, D_MODEL))


def kernel(x, c, w_ada, b_ada, w_in, w_pool_mix, pool_scale, w_att_out, w_pool_out, w_o, ln1_g, ln1_b,
           w_ffn_in, w_ffn_out, ln2_g, ln2_b):
    batch, seq, d = x.shape
    tm = 512
    att_tile = RATES[-1] * HALF_SPAN
    assert d == D_MODEL and seq % att_tile == 0 and seq // att_tile >= 2
    mod = _modulation(c, w_ada, b_ada)
    xf = x.reshape(batch * seq, d)
    for layer in range(DEPTH):
        outs = _in_projection(xf, mod, w_in[layer].astype(BF16), layer, batch, seq, tm)
        qkv, pool_in, gates = outs[:3 * N_GROUPS], outs[3 * N_GROUPS], outs[3 * N_GROUPS + 1]
        att = _attention(qkv, batch, seq, att_tile)
        xf = _mix_epilogue(att, pool_in, gates, xf, mod, layer, seq, tm,
                           w_att_out[layer].astype(BF16), w_pool_mix[layer].astype(BF16),
                           w_pool_out[layer].astype(BF16), w_o[layer].astype(BF16), pool_scale[layer],
                           ln1_g[layer], ln1_b[layer])
        xf = _ffn(xf, mod, layer, seq, tm, w_ffn_in[layer].astype(BF16), w_ffn_out[layer].astype(BF16),
                  ln2_g[layer], ln2_b[layer])
    return xf.reshape(batch, seq, d)
```

```python
import functools

import jax
import jax.numpy as jnp
import numpy as np
from jax import lax
from jax.experimental import pallas as pl
from jax.experimental.pallas import tpu as pltpu

F32 = jnp.float32
BF16 = jnp.bfloat16

D_MODEL = 1024
DEPTH = 2
HEAD_DIM = 64
N_SLOTS = 8
ATT_WIDTH = N_SLOTS * HEAD_DIM
RATES = (1, 4, 16)
N_GROUPS = len(RATES)
HALF_SPAN = 64
QKV_WIDTH = N_GROUPS * 3 * ATT_WIDTH
ALIBI_MAX = 8.0
POOL_WINDOWS = (2, 4, 8, 16)
POOL_GROUP = 128
POOL_WIDTH = len(POOL_WINDOWS) * POOL_GROUP
POOL_HALO = max(POOL_WINDOWS) // 2
IN_WIDTH = QKV_WIDTH + POOL_WIDTH + 2 * D_MODEL
D_FF = 2816
ALPHA = (2 * DEPTH) ** 0.25
LN_EPS = 1e-5
N_MOD = 6
MASK_VALUE = -1e30
LOG2_E = 1.4426950408889634
Q_SCALE = HEAD_DIM ** -0.5 * LOG2_E

LANES = 128
N_PAIRS = ATT_WIDTH // LANES
Q_BLOCK = 128
K_BLOCK = Q_BLOCK + 2 * HALF_SPAN
SCORE_LEAD = 4
MOD_ROWS = 8
V7X_VMEM_LIMIT = 56 * 1024 * 1024


def _dot(a, b):
    return jnp.dot(a, b, preferred_element_type=F32)


def _layer_norm(x):
    mu = jnp.mean(x, axis=-1, keepdims=True)
    xc = x - mu
    var = jnp.mean(xc * xc, axis=-1, keepdims=True)
    return xc * lax.rsqrt(var + LN_EPS)


def _resident(shape):
    zeros = (0,) * len(shape)
    return pl.BlockSpec(shape, lambda *_: zeros, pipeline_mode=pl.Buffered(1))


def _layer_block(shape, layer):
    zeros = (0,) * (len(shape) - 1)
    return pl.BlockSpec((1,) + tuple(shape[1:]), lambda *_: (layer,) + zeros, pipeline_mode=pl.Buffered(1))


def _mod_kernel(c_ref, w_ref, b_ref, o_ref):
    c = c_ref[...]
    s = c * jax.nn.sigmoid(c)
    w = w_ref[0]
    s_hi = s.astype(BF16)
    s_lo = (s - s_hi.astype(F32)).astype(BF16)
    w_hi = w.astype(BF16)
    w_lo = (w - w_hi.astype(F32)).astype(BF16)
    acc = _dot(s_hi, w_hi) + _dot(s_hi, w_lo) + _dot(s_lo, w_hi)
    o_ref[0] = acc + b_ref[0]


def _modulation(c, w_ada, b_ada):
    batch, d = c.shape
    c_pad = jnp.pad(c, ((0, MOD_ROWS - batch), (0, 0)))
    b3 = b_ada.reshape(DEPTH * N_MOD, 1, d)
    mod = pl.pallas_call(
        _mod_kernel,
        grid=(DEPTH, N_MOD),
        in_specs=[
            pl.BlockSpec((MOD_ROWS, d), lambda l, k: (0, 0)),
            pl.BlockSpec((1, d, d), lambda l, k: (l, 0, k)),
            pl.BlockSpec((1, 1, d), lambda l, k: (l * N_MOD + k, 0, 0)),
        ],
        out_specs=pl.BlockSpec((1, MOD_ROWS, d), lambda l, k: (l * N_MOD + k, 0, 0)),
        out_shape=jax.ShapeDtypeStruct((DEPTH * N_MOD, MOD_ROWS, d), F32),
        compiler_params=pltpu.CompilerParams(dimension_semantics=("arbitrary", "arbitrary")),
        name="adaln_mod",
    )(c_pad, w_ada, b3)
    return mod.reshape(DEPTH * N_MOD * MOD_ROWS, 1, d)


def _mod_spec(layer, k, rows_per_batch_tiles):
    base = (layer * N_MOD + k) * MOD_ROWS
    return pl.BlockSpec((1, 1, D_MODEL), lambda i: (base + i // rows_per_batch_tiles, 0, 0))


def _inproj_kernel(x_ref, shift_ref, scale_ref, w_ref, *refs, tm):
    qkv_refs = refs[: 3 * N_GROUPS]
    pool_ref, gates_ref, hs_ref = refs[3 * N_GROUPS:3 * N_GROUPS + 3]
    h_refs = refs[3 * N_GROUPS + 3:]
    h = _layer_norm(x_ref[...]) * (1.0 + scale_ref[0]) + shift_ref[0]
    h_refs[0][...] = h.astype(BF16)
    n_slabs = D_MODEL // LANES
    for s in range(n_slabs):
        hs_ref[s] = h[:, s * LANES:(s + 1) * LANES]
    for g in range(1, N_GROUPS):
        rate = RATES[g]
        n = tm // rate
        for p in range(rate):
            for s in range(n_slabs):
                h_refs[g][p * n:(p + 1) * n, s * LANES:(s + 1) * LANES] = (
                    hs_ref[s, pl.ds(p, n, stride=rate), :].astype(BF16))
    for j, o_ref in enumerate(qkv_refs):
        g = j // 3
        rate = RATES[g]
        n = tm // rate
        r = _dot(h_refs[g][...], w_ref[0, :, j * ATT_WIDTH:(j + 1) * ATT_WIDTH])
        if j % 3 == 0:
            r = r * Q_SCALE
        r = r.astype(BF16)
        for p in range(rate):
            o_ref[0, p] = r[p * n:(p + 1) * n]
    pool_ref[...] = _dot(h_refs[0][...], w_ref[0, :, QKV_WIDTH:QKV_WIDTH + POOL_WIDTH])
    g0 = QKV_WIDTH + POOL_WIDTH
    for j in range(2 * D_MODEL // ATT_WIDTH):
        gates_ref[:, j * ATT_WIDTH:(j + 1) * ATT_WIDTH] = _dot(
            h_refs[0][...], w_ref[0, :, g0 + j * ATT_WIDTH:g0 + (j + 1) * ATT_WIDTH])


def _in_projection(xf, mod, w_in_bf, layer, batch, seq, tm):
    tokens = xf.shape[0]
    tiles_per_seq = seq // tm
    row = lambda width: pl.BlockSpec((tm, width), lambda i: (i, 0))
    phase_major = lambda rate: pl.BlockSpec(
        (1, rate, tm // rate, ATT_WIDTH), lambda i: (i // tiles_per_seq, 0, i % tiles_per_seq, 0))
    out_shape, out_specs = [], []
    for rate in RATES:
        out_shape += [jax.ShapeDtypeStruct((batch, rate, seq // rate, ATT_WIDTH), BF16)] * 3
        out_specs += [phase_major(rate)] * 3
    out_shape += [jax.ShapeDtypeStruct((tokens, POOL_WIDTH), F32),
                  jax.ShapeDtypeStruct((tokens, 2 * D_MODEL), F32)]
    out_specs += [row(POOL_WIDTH), row(2 * D_MODEL)]
    return pl.pallas_call(
        functools.partial(_inproj_kernel, tm=tm),
        grid=(tokens // tm,),
        in_specs=[row(D_MODEL), _mod_spec(layer, 0, tiles_per_seq), _mod_spec(layer, 1, tiles_per_seq),
                  _layer_block(w_in_bf.shape, layer)],
        out_specs=out_specs,
        out_shape=out_shape,
        scratch_shapes=[pltpu.VMEM((D_MODEL // LANES, tm, LANES), F32)]
        + [pltpu.VMEM((tm, D_MODEL), BF16)] * N_GROUPS,
        compiler_params=pltpu.CompilerParams(dimension_semantics=("arbitrary",),
                                             vmem_limit_bytes=V7X_VMEM_LIMIT),
        name="in_projection",
    )(xf, mod, mod, w_in_bf)


def _attention_bias(group, n_q):
    rate = RATES[group]
    n = N_GROUPS * N_SLOTS
    slopes = np.exp2(-ALIBI_MAX * np.arange(1, n + 1, dtype=np.float32) / n).reshape(N_GROUPS, N_SLOTS)
    col = np.arange(K_BLOCK)[None, :]
    rel = col - HALF_SPAN - np.arange(n_q)[:, None]
    band = np.abs(rel) <= HALF_SPAN
    dist = (rate * np.abs(rel)).astype(np.float32)
    edge = [col >= HALF_SPAN, col >= 0, col < n_q + HALF_SPAN]
    out = np.empty((3, N_PAIRS, 2 * n_q, K_BLOCK), np.float32)
    for e in range(3):
        for h in range(N_SLOTS):
            b = np.where(band & edge[e], -slopes[group, h] * dist * np.float32(LOG2_E), np.float32(MASK_VALUE))
            out[e, h // 2, (h % 2) * n_q:(h % 2 + 1) * n_q] = b
    return jnp.asarray(out)


def _scores(q2, kk, bias, low):
    zero = jnp.zeros_like(q2)
    qq = jnp.concatenate([jnp.where(low, q2, zero), jnp.where(low, zero, q2)], axis=0)
    return lax.dot_general(qq, kk, (((1,), (1,)), ((), ())), preferred_element_type=F32) + bias


def _softmax_pv(s, vv, low):
    n = s.shape[0] // 2
    m = jnp.max(s, axis=-1, keepdims=True)
    p = jnp.exp2(s - m)
    denom = jnp.sum(p, axis=-1, keepdims=True)
    pv = _dot(p.astype(BF16), vv)
    wide = lambda c: jnp.where(low, jnp.broadcast_to(c[:n], (n, LANES)), jnp.broadcast_to(c[n:], (n, LANES)))
    return jnp.where(low, pv[:n], pv[n:]), wide(m), wide(denom)


def _attn_kernel(*refs, tile, tiles_per_seq):
    group_refs = [refs[8 * g:8 * g + 8] for g in range(N_GROUPS)]
    o_ref, acc_ref, max_ref, sum_ref = refs[8 * N_GROUPS:]
    t = pl.program_id(1)
    first_tile = t == 0
    last_tile = t == tiles_per_seq - 1

    def window(main_ref, prev_ref, next_ref, p, a, cols, lt):
        start, end = a - HALF_SPAN, a - HALF_SPAN + K_BLOCK
        pieces = []
        if start < 0:
            pieces.append(prev_ref[0, p, :, cols])
        pieces.append(main_ref[0, p, max(start, 0):min(end, lt), cols])
        if end > lt:
            pieces.append(next_ref[0, p, 0:min(end - lt, HALF_SPAN), cols])
            if end - lt > HALF_SPAN:
                pieces.append(jnp.zeros((end - lt - HALF_SPAN, LANES), BF16))
        return pieces[0] if len(pieces) == 1 else jnp.concatenate(pieces, axis=0)

    group_order = tuple(reversed(range(N_GROUPS)))
    chains = []
    for g in group_order:
        rate = RATES[g]
        lt = tile // rate
        n_q = min(Q_BLOCK, lt)
        n_blocks = lt // n_q
        for p in range(rate):
            for j in range(n_blocks):
                for pair in range(N_PAIRS):
                    chains.append((g, rate, lt, n_q, n_blocks, p, j, pair))

    def start(chain):
        g, rate, lt, n_q, n_blocks, p, j, pair = chain
        q_ref, k_ref, kp_ref, kn_ref, _, _, _, bias_ref = group_refs[g]
        a = j * n_q
        cols = slice(pair * LANES, (pair + 1) * LANES)
        low = lax.broadcasted_iota(jnp.int32, (n_q, LANES), 1) < HEAD_DIM
        if j == 0 and j == n_blocks - 1:
            variant = jnp.where(first_tile, 0, jnp.where(last_tile, 2, 1))
        elif j == 0:
            variant = jnp.where(first_tile, 0, 1)
        elif j == n_blocks - 1:
            variant = jnp.where(last_tile, 2, 1)
        else:
            variant = 1
        kk = window(k_ref, kp_ref, kn_ref, p, a, cols, lt)
        return _scores(q_ref[0, p, a:a + n_q, cols], kk, bias_ref[variant, pair], low)

    def finish(chain, s):
        g, rate, lt, n_q, n_blocks, p, j, pair = chain
        _, _, _, _, v_ref, vp_ref, vn_ref, _ = group_refs[g]
        a = j * n_q
        cols = slice(pair * LANES, (pair + 1) * LANES)
        low = lax.broadcasted_iota(jnp.int32, (n_q, LANES), 1) < HEAD_DIM
        u, m, d = _softmax_pv(s, window(v_ref, vp_ref, vn_ref, p, a, cols, lt), low)
        rows = pl.ds(a, n_q) if rate == 1 else pl.ds(a * rate + p, n_q, stride=rate)
        if g == group_order[0]:
            acc_ref[pair, rows, :] = u
            max_ref[pair, rows, :] = m
            sum_ref[pair, rows, :] = d
            return
        m_old = max_ref[pair, rows, :]
        m_new = jnp.maximum(m_old, m)
        w_old = jnp.exp2(m_old - m_new)
        w_new = jnp.exp2(m - m_new)
        acc = acc_ref[pair, rows, :] * w_old + u * w_new
        total = sum_ref[pair, rows, :] * w_old + d * w_new
        if g == group_order[-1]:
            assert rate == 1
            o_ref[rows, cols] = (acc / total).astype(BF16)
        else:
            acc_ref[pair, rows, :] = acc
            sum_ref[pair, rows, :] = total
            max_ref[pair, rows, :] = m_new

    pending = {}
    for i in range(len(chains) + SCORE_LEAD):
        if i < len(chains):
            pending[i] = start(chains[i])
        if i >= SCORE_LEAD:
            finish(chains[i - SCORE_LEAD], pending.pop(i - SCORE_LEAD))


def _attention(qkv, batch, seq, tile):
    tiles_per_seq = seq // tile
    in_specs, args = [], []
    for g, rate in enumerate(RATES):
        lt = tile // rate
        length = seq // rate
        halo_per_tile = lt // HALF_SPAN
        main = pl.BlockSpec((1, rate, lt, ATT_WIDTH), lambda b, t: (b, 0, t, 0))
        prev = pl.BlockSpec((1, rate, HALF_SPAN, ATT_WIDTH),
                            lambda b, t, h=halo_per_tile: (b, 0, jnp.maximum(t * h - 1, 0), 0))
        nxt = pl.BlockSpec((1, rate, HALF_SPAN, ATT_WIDTH),
                           lambda b, t, h=halo_per_tile, last=length // HALF_SPAN - 1:
                           (b, 0, jnp.minimum((t + 1) * h, last), 0))
        q, k, v = qkv[3 * g:3 * g + 3]
        bias = _attention_bias(g, min(Q_BLOCK, lt))
        in_specs += [main, main, prev, nxt, main, prev, nxt, _resident(bias.shape)]
        args += [q, k, k, k, v, v, v, bias]
    return pl.pallas_call(
        functools.partial(_attn_kernel, tile=tile, tiles_per_seq=tiles_per_seq),
        grid=(batch, tiles_per_seq),
        in_specs=in_specs,
        out_specs=pl.BlockSpec((tile, ATT_WIDTH), lambda b, t: (b * tiles_per_seq + t, 0)),
        out_shape=jax.ShapeDtypeStruct((batch * seq, ATT_WIDTH), BF16),
        scratch_shapes=[pltpu.VMEM((N_PAIRS, tile, LANES), F32)] * 3,
        compiler_params=pltpu.CompilerParams(dimension_semantics=("arbitrary", "arbitrary"),
                                             vmem_limit_bytes=V7X_VMEM_LIMIT),
        name="dilated_attention",
    )(*args)


def _mix_stage(ti, att_ref, pm_ref, pp_ref, pn_ref, gates_ref, x_ref, gate_ref, wao_ref, wpm_ref, wpo_ref,
               wo_ref, ps_ref, lng_ref, lnb_ref, ext_ref, *, tm, seq):
    tiles_per_seq = seq // tm
    y_att = _dot(att_ref[...], wao_ref[0])

    halo_zero = jnp.zeros((POOL_HALO, POOL_WIDTH), F32)
    ext_ref[0:POOL_HALO, :] = jnp.where(ti == 0, halo_zero, pp_ref[...])
    ext_ref[POOL_HALO:POOL_HALO + tm, :] = pm_ref[...]
    ext_ref[POOL_HALO + tm:, :] = jnp.where(ti == tiles_per_seq - 1, halo_zero, pn_ref[...])
    pos = ti * tm + lax.broadcasted_iota(jnp.int32, (tm, POOL_GROUP), 0)
    pooled = []
    for g, window in enumerate(POOL_WINDOWS):
        half = window // 2
        cols = slice(g * POOL_GROUP, (g + 1) * POOL_GROUP)
        rows = tm + (2 * POOL_HALO if half == POOL_HALO else POOL_HALO)
        run = ext_ref[POOL_HALO - half:POOL_HALO - half + rows, cols]
        span = 1
        while span < window:
            run = run + pltpu.roll(run, rows - span, axis=0)
            span *= 2
        count = (jnp.minimum(pos + half, seq) - jnp.maximum(pos - half, 0)).astype(F32)
        pooled.append((run[:tm] / count - pm_ref[:, cols]).astype(BF16))
    yield
    mixed = [_dot(pooled[g], wpm_ref[0, g]) * ps_ref[0][:, g * POOL_GROUP:(g + 1) * POOL_GROUP]
             for g in range(len(POOL_WINDOWS))]
    y_pool = _dot(jnp.concatenate(mixed, axis=1).astype(BF16), wpo_ref[0])
    yield
    merged = (jax.nn.sigmoid(gates_ref[:, :D_MODEL]) * y_att
              + jax.nn.sigmoid(gates_ref[:, D_MODEL:]) * y_pool)
    mix_out = _dot(merged.astype(BF16), wo_ref[0])
    yield
    z = ALPHA * x_ref[...] + (1.0 + gate_ref[0]) * mix_out
    return _layer_norm(z) * lng_ref[0] + lnb_ref[0]


FFN_CHUNK = 256


def _ffn_stage(x, shift_ref, scale_ref, gate_ref, w1_ref, w2_ref, lng_ref, lnb_ref, h_ref, act_ref):
    h_ref[...] = (_layer_norm(x) * (1.0 + scale_ref[0]) + shift_ref[0]).astype(BF16)
    yield
    for c in range(D_FF // FFN_CHUNK):
        lo = c * FFN_CHUNK
        a = _dot(h_ref[...], w1_ref[0, :, lo:lo + FFN_CHUNK])
        b = _dot(h_ref[...], w1_ref[0, :, D_FF + lo:D_FF + lo + FFN_CHUNK])
        act_ref[:, lo:lo + FFN_CHUNK] = (a * jax.nn.sigmoid(a) * b).astype(BF16)
        yield
    ffn_out = _dot(act_ref[...], w2_ref[0])
    z = ALPHA * x + (1.0 + gate_ref[0]) * ffn_out
    return _layer_norm(z) * lng_ref[0] + lnb_ref[0]


MIX_AFTER_FFN_SEGMENT = (0, 3, 6, 9)


def _run_interleaved(main, side, side_after):
    done = {}

    def advance(gen, key):
        if key in done:
            return False
        try:
            next(gen)
            return True
        except StopIteration as stop:
            done[key] = stop.value
            return False

    k = 0
    while True:
        if k > max(side_after):
            while advance(side, "side"):
                pass
        if not advance(main, "main"):
            break
        if k in side_after:
            advance(side, "side")
        k += 1
    return done["main"], done["side"]


def _mix_ffn_kernel(att_ref, pm_ref, pp_ref, pn_ref, gates_ref, x_ref, gate1_ref, wao_ref, wpm_ref, wpo_ref,
                    wo_ref, ps_ref, ln1g_ref, ln1b_ref, shift2_ref, scale2_ref, gate2_ref, w1_ref, w2_ref,
                    ln2g_ref, ln2b_ref, out_ref, ext_ref, cur_ref, nxt_ref, h_ref, act_ref, *, tm, seq,
                    n_tiles):
    i = pl.program_id(0)

    @pl.when(i == 0)
    def _():
        cur_ref[...] = jnp.zeros_like(cur_ref)

    ti = jnp.minimum(i, n_tiles - 1) % (seq // tm)
    ffn = _ffn_stage(cur_ref[...], shift2_ref, scale2_ref, gate2_ref, w1_ref, w2_ref, ln2g_ref, ln2b_ref,
                     h_ref, act_ref)
    mix = _mix_stage(ti, att_ref, pm_ref, pp_ref, pn_ref, gates_ref, x_ref, gate1_ref, wao_ref, wpm_ref,
                     wpo_ref, wo_ref, ps_ref, ln1g_ref, ln1b_ref, ext_ref, tm=tm, seq=seq)
    out_ref[...], nxt_ref[...] = _run_interleaved(ffn, mix, MIX_AFTER_FFN_SEGMENT)
    cur_ref[...] = nxt_ref[...]


def _mix_ffn(att, pool_in, gates, xf, mod, layer, seq, tm, wao, wpm, wpo, wo, pool_scale, ln1_g, ln1_b,
             w1, w2, ln2_g, ln2_b):
    tokens = xf.shape[0]
    n_tiles = tokens // tm
    tiles_per_seq = seq // tm
    halo_per_tile = tm // POOL_HALO
    mix_tile = lambda i: jnp.minimum(i, n_tiles - 1)
    ffn_tile = lambda i: jnp.maximum(i - 1, 0)
    row = lambda width: pl.BlockSpec((tm, width), lambda i: (mix_tile(i), 0))
    prev = pl.BlockSpec((POOL_HALO, POOL_WIDTH),
                        lambda i: (jnp.maximum(mix_tile(i) * halo_per_tile - 1, 0), 0))
    nxt = pl.BlockSpec((POOL_HALO, POOL_WIDTH),
                       lambda i: (jnp.minimum((mix_tile(i) + 1) * halo_per_tile, tokens // POOL_HALO - 1), 0))

    def mod_spec(k, tile_of):
        base = (layer * N_MOD + k) * MOD_ROWS
        return pl.BlockSpec((1, 1, D_MODEL), lambda i: (base + tile_of(i) // tiles_per_seq, 0, 0))

    vec = lambda a: a.reshape(DEPTH, 1, a.shape[-1])
    params1 = (wao, wpm, wpo, wo, vec(pool_scale), vec(ln1_g), vec(ln1_b))
    params2 = (w1, w2, vec(ln2_g), vec(ln2_b))
    return pl.pallas_call(
        functools.partial(_mix_ffn_kernel, tm=tm, seq=seq, n_tiles=n_tiles),
        grid=(n_tiles + 1,),
        in_specs=[row(ATT_WIDTH), row(POOL_WIDTH), prev, nxt, row(2 * D_MODEL), row(D_MODEL),
                  mod_spec(2, mix_tile)] + [_layer_block(p.shape, layer) for p in params1]
        + [mod_spec(3, ffn_tile), mod_spec(4, ffn_tile), mod_spec(5, ffn_tile)]
        + [_layer_block(p.shape, layer) for p in params2],
        out_specs=pl.BlockSpec((tm, D_MODEL), lambda i: (ffn_tile(i), 0)),
        out_shape=jax.ShapeDtypeStruct((tokens, D_MODEL), F32),
        scratch_shapes=[pltpu.VMEM((tm + 2 * POOL_HALO, POOL_WIDTH), F32), pltpu.VMEM((tm, D_MODEL), F32),
                        pltpu.VMEM((tm, D_MODEL), F32), pltpu.VMEM((tm, D_MODEL), BF16),
                        pltpu.VMEM((tm, D_FF), BF16)],
        compiler_params=pltpu.CompilerParams(dimension_semantics=("arbitrary",),
                                             vmem_limit_bytes=V7X_VMEM_LIMIT),
        name="mix_ffn",
    )(att, pool_in, pool_in, pool_in, gates, xf, mod, *params1, mod, mod, mod, *params2)


def kernel(x, c, w_ada, b_ada, w_in, w_pool_mix, pool_scale, w_att_out, w_pool_out, w_o, ln1_g, ln1_b,
           w_ffn_in, w_ffn_out, ln2_g, ln2_b):
    batch, seq, d = x.shape
    tm = 512
    att_tile = RATES[-1] * HALF_SPAN
    assert d == D_MODEL and seq % att_tile == 0 and seq // att_tile >= 2
    mod = _modulation(c, w_ada, b_ada)
    xf = x.reshape(batch * seq, d)
    w_in, w_att_out, w_pool_mix, w_pool_out, w_o, w_ffn_in, w_ffn_out = (
        w.astype(BF16) for w in (w_in, w_att_out, w_pool_mix, w_pool_out, w_o, w_ffn_in, w_ffn_out))
    for layer in range(DEPTH):
        outs = _in_projection(xf, mod, w_in, layer, batch, seq, tm)
        qkv, pool_in, gates = outs[:3 * N_GROUPS], outs[3 * N_GROUPS], outs[3 * N_GROUPS + 1]
        att = _attention(qkv, batch, seq, att_tile)
        xf = _mix_ffn(att, pool_in, gates, xf, mod, layer, seq, tm, w_att_out, w_pool_mix, w_pool_out, w_o,
                      pool_scale, ln1_g, ln1_b, w_ffn_in, w_ffn_out, ln2_g, ln2_b)
    return xf.reshape(batch, seq, d)
```
